```python
import jax, jax.numpy as jnp
from jax import lax
import numpy as np

D_MODEL = 2048
BATCH = 2
SEQ = 16384
DEPTH = 1

MEM_LEN = 256
EPS = 1e-6
POOL_WINDOWS = (2, 4, 8, 16)
POOL_GROUP_DIM = D_MODEL // 8
POOL_DIM = POOL_GROUP_DIM * len(POOL_WINDOWS)
HGRN_DK = 128
HGRN_DV = 128
HGRN_HEADS = D_MODEL // 256
HGRN_DIM = HGRN_HEADS * HGRN_DK
MIX_DIM = POOL_DIM + HGRN_DIM
IN_DIM = POOL_DIM + 4 * HGRN_DIM
CHUNK = 64
XATTN_HEADS = 4
XATTN_HEAD_DIM = D_MODEL // XATTN_HEADS
N_GROUPS = 4
EXPERTS_PER_GROUP = 4
N_EXPERTS = N_GROUPS * EXPERTS_PER_GROUP
TOP_K_INNER = 2
D_EXPERT = D_MODEL // 4

kernel_name = "hybrid_pool_hgrn2_memxattn_hmoe"


def rmsnorm(x, gain):
    xf = x.astype(jnp.float32)
    y = xf * lax.rsqrt(jnp.mean(xf * xf, axis=-1, keepdims=True) + EPS)
    return (y * gain.astype(jnp.float32)).astype(x.dtype)


def causal_pool_mixer(u, pool_w, pool_scale):
    b, s, _ = u.shape
    uf = u.astype(jnp.float32)
    cs = jnp.concatenate([jnp.zeros((b, 1, POOL_DIM), jnp.float32), jnp.cumsum(uf, axis=1)], axis=1)
    t_plus_1 = jnp.arange(1, s + 1, dtype=jnp.int32)
    outs = []
    for g, w in enumerate(POOL_WINDOWS):
        sl = slice(g * POOL_GROUP_DIM, (g + 1) * POOL_GROUP_DIM)
        c = cs[:, :, sl]
        upper = c[:, 1:]
        lower = jnp.pad(c[:, : s + 1 - w], ((0, 0), (w - 1, 0), (0, 0)))
        count = jnp.minimum(t_plus_1, w).astype(jnp.float32)[None, :, None]
        pooled = (upper - lower) / count - uf[:, :, sl]
        outs.append(jnp.einsum('bsc,cd->bsd', pooled.astype(u.dtype), pool_w[g]))
    return (jnp.concatenate(outs, axis=-1) * pool_scale).astype(u.dtype)


def hgrn2_mixer(q_in, f_in, i_in, g_in, lb, norm_gain):
    b, s, _ = q_in.shape
    n_chunks = s // CHUNK
    q = jax.nn.silu(q_in.astype(jnp.float32))
    lb = lb.astype(jnp.float32)
    log_f = jnp.logaddexp(jnp.log(lb), jnp.log1p(-lb) + jax.nn.log_sigmoid(f_in.astype(jnp.float32)))
    k = -jnp.expm1(log_f)
    v = i_in.astype(jnp.float32)

    def to_chunks(t, d):
        return t.reshape(b, n_chunks, CHUNK, HGRN_HEADS, d).transpose(1, 0, 3, 2, 4)

    causal = jnp.tril(jnp.ones((CHUNK, CHUNK), dtype=bool))[None, None, :, :, None]

    def step(state, xs):
        qc, lfc, kc, vc = xs
        a = jnp.cumsum(lfc, axis=2)
        rel = a[:, :, :, None, :] - a[:, :, None, :, :]
        decay = jnp.exp(jnp.where(causal, rel, -jnp.inf))
        scores = jnp.einsum('bhtd,bhtsd,bhsd->bhts', qc, decay, kc)
        o = jnp.einsum('bhts,bhsv->bhtv', scores, vc) + jnp.einsum('bhtd,bhdv->bhtv', qc * jnp.exp(a), state)
        a_last = a[:, :, -1:, :]
        new_state = jnp.exp(a_last[:, :, 0, :])[..., None] * state + jnp.einsum(
            'bhsd,bhsv->bhdv', kc * jnp.exp(a_last - a), vc)
        return new_state, o

    s0 = jnp.zeros((b, HGRN_HEADS, HGRN_DK, HGRN_DV), jnp.float32)
    _, o = lax.scan(step, s0, (to_chunks(q, HGRN_DK), to_chunks(log_f, HGRN_DK),
                               to_chunks(k, HGRN_DK), to_chunks(v, HGRN_DV)))
    o = o.transpose(1, 0, 3, 2, 4).reshape(b, s, HGRN_HEADS, HGRN_DV)
    o = rmsnorm(o, norm_gain)
    gate = jax.nn.silu(g_in.astype(jnp.float32)).reshape(b, s, HGRN_HEADS, HGRN_DV)
    return (o * gate).reshape(b, s, HGRN_HEADS * HGRN_DV).astype(q_in.dtype)


def memory_cross_attention(h, mem_n, wq, wk, wv, wo):
    b, s, _ = h.shape
    m = mem_n.shape[1]
    q = (h @ wq).reshape(b, s, XATTN_HEADS, XATTN_HEAD_DIM)
    k = (mem_n @ wk).reshape(b, m, XATTN_HEADS, XATTN_HEAD_DIM)
    v = (mem_n @ wv).reshape(b, m, XATTN_HEADS, XATTN_HEAD_DIM)
    scores = jnp.einsum('bshd,bmhd->bhsm', q, k).astype(jnp.float32) * (XATTN_HEAD_DIM ** -0.5)
    p = jax.nn.softmax(scores, axis=-1).astype(v.dtype)
    o = jnp.einsum('bhsm,bmhd->bshd', p, v).reshape(b, s, D_MODEL)
    return o @ wo


def hierarchical_moe(h, router_group, router_expert, w_gate, w_up, w_down):
    b, s, _ = h.shape
    p_group = jax.nn.softmax(jnp.einsum('bsd,dg->bsg', h, router_group).astype(jnp.float32), axis=-1)
    pg_top, g_idx = lax.top_k(p_group, 1)
    logits_e = jnp.einsum('bsd,de->bse', h, router_expert).astype(jnp.float32)
    logits_e = logits_e.reshape(b, s, N_GROUPS, EXPERTS_PER_GROUP)
    g_onehot = jax.nn.one_hot(g_idx[..., 0], N_GROUPS, dtype=jnp.float32)
    sel = jnp.einsum('bsg,bsge->bse', g_onehot, logits_e)
    top_vals, top_idx = lax.top_k(sel, TOP_K_INNER)
    w_inner = jax.nn.softmax(top_vals, axis=-1)
    expert_id = g_idx * EXPERTS_PER_GROUP + top_idx
    combine = jnp.sum(jax.nn.one_hot(expert_id, N_EXPERTS, dtype=jnp.float32)
                      * (pg_top * w_inner)[..., None], axis=2).astype(h.dtype)
    y = jnp.zeros_like(h)
    for e in range(N_EXPERTS):
        hid = jax.nn.silu(h @ w_gate[e]) * (h @ w_up[e])
        y = y + combine[..., e:e + 1] * (hid @ w_down[e])
    return y


def setup_inputs(seed: int = 0) -> dict:
    key = jax.random.key(seed)
    ks = jax.random.split(key, 24)
    f32 = jnp.float32

    def nrm(k, shape, scale):
        return jax.random.normal(k, shape, f32) * scale

    def gain(k, shape):
        return 1.0 + 0.02 * jax.random.normal(k, shape, f32)

    return {
        "x": nrm(ks[0], (BATCH, SEQ, D_MODEL), 1.0),
        "mem": nrm(ks[1], (BATCH, MEM_LEN, D_MODEL), 1.0),
        "norm_mix": gain(ks[2], (DEPTH, D_MODEL)),
        "w_in": nrm(ks[3], (DEPTH, D_MODEL, IN_DIM), D_MODEL ** -0.5),
        "pool_w": nrm(ks[4], (DEPTH, len(POOL_WINDOWS), POOL_GROUP_DIM, POOL_GROUP_DIM), POOL_GROUP_DIM ** -0.5),
        "pool_scale": gain(ks[5], (DEPTH, POOL_DIM)),
        "hgrn_lb_logits": nrm(ks[6], (DEPTH + 1, HGRN_DIM), 0.5),
        "hgrn_norm": gain(ks[7], (DEPTH, HGRN_DV)),
        "w_out": nrm(ks[8], (DEPTH, MIX_DIM, D_MODEL), MIX_DIM ** -0.5),
        "norm_xattn": gain(ks[9], (DEPTH, D_MODEL)),
        "norm_mem": gain(ks[10], (DEPTH, D_MODEL)),
        "xattn_wq": nrm(ks[11], (DEPTH, D_MODEL, D_MODEL), D_MODEL ** -0.5),
        "xattn_wk": nrm(ks[12], (DEPTH, D_MODEL, D_MODEL), D_MODEL ** -0.5),
        "xattn_wv": nrm(ks[13], (DEPTH, D_MODEL, D_MODEL), D_MODEL ** -0.5),
        "xattn_wo": nrm(ks[14], (DEPTH, D_MODEL, D_MODEL), D_MODEL ** -0.5),
        "norm_ffn": gain(ks[15], (DEPTH, D_MODEL)),
        "router_group": nrm(ks[16], (DEPTH, D_MODEL, N_GROUPS), D_MODEL ** -0.5),
        "router_expert": nrm(ks[17], (DEPTH, D_MODEL, N_EXPERTS), D_MODEL ** -0.5),
        "w_gate": nrm(ks[18], (DEPTH, N_EXPERTS, D_MODEL, D_EXPERT), D_MODEL ** -0.5),
        "w_up": nrm(ks[19], (DEPTH, N_EXPERTS, D_MODEL, D_EXPERT), D_MODEL ** -0.5),
        "w_down": nrm(ks[20], (DEPTH, N_EXPERTS, D_EXPERT, D_MODEL), D_EXPERT ** -0.5),
        "norm_final": gain(ks[21], (D_MODEL,)),
    }


def reference(x, mem, norm_mix, w_in, pool_w, pool_scale, hgrn_lb_logits, hgrn_norm, w_out,
              norm_xattn, norm_mem, xattn_wq, xattn_wk, xattn_wv, xattn_wo, norm_ffn,
              router_group, router_expert, w_gate, w_up, w_down, norm_final):
    lower_bounds = jnp.cumsum(jax.nn.softmax(hgrn_lb_logits.astype(jnp.float32), axis=0), axis=0)
    splits = [POOL_DIM, POOL_DIM + HGRN_DIM, POOL_DIM + 2 * HGRN_DIM, POOL_DIM + 3 * HGRN_DIM]
    for l in range(DEPTH):
        h = rmsnorm(x, norm_mix[l])
        proj = h @ w_in[l]
        u, q_in, f_in, i_in, g_in = jnp.split(proj, splits, axis=-1)
        pool_out = causal_pool_mixer(u, pool_w[l], pool_scale[l])
        hgrn_out = hgrn2_mixer(q_in, f_in, i_in, g_in, lower_bounds[l], hgrn_norm[l])
        x = x + jnp.concatenate([pool_out, hgrn_out], axis=-1) @ w_out[l]
        h = rmsnorm(x, norm_xattn[l])
        m = rmsnorm(mem, norm_mem[l])
        x = x + memory_cross_attention(h, m, xattn_wq[l], xattn_wk[l], xattn_wv[l], xattn_wo[l])
        h = rmsnorm(x, norm_ffn[l])
        x = x + hierarchical_moe(h, router_group[l], router_expert[l], w_gate[l], w_up[l], w_down[l])
    return rmsnorm(x, norm_final)
```

```python
import functools

import jax
import jax.numpy as jnp
from jax import lax
from jax.experimental import pallas as pl
from jax.experimental.pallas import tpu as pltpu

F32 = jnp.float32
BF16 = jnp.bfloat16

D_MODEL = 2048
EPS = 1e-6
POOL_WINDOWS = (2, 4, 8, 16)
POOL_GROUP = 256
POOL_DIM = POOL_GROUP * len(POOL_WINDOWS)
HEADS = 8
DK = 128
HGRN_DIM = HEADS * DK
IN_DIM = POOL_DIM + 4 * HGRN_DIM
XH = 4
XHD = D_MODEL // XH
N_GROUPS = 4
EPG = 4
N_EXPERTS = N_GROUPS * EPG
D_EXPERT = D_MODEL // 4

V7X_VMEM_LIMIT_BYTES = 56 * 2**20
SUBLANES = 8
TM_MIX = 256
CH = 128
HALO = max(POOL_WINDOWS)
LEVEL_BLOCKS = (128, 64, 32, 16, 8, 4, 2)
TM_ATT = 256
TM_MOE = 512
ROUTER_ROWS = 32


def _dot(a, b):
    return jnp.dot(a, b, preferred_element_type=F32)


def _dot_nt(a, b):
    return lax.dot_general(a, b, (((1,), (1,)), ((), ())), preferred_element_type=F32)


def _dot_tn(a, b):
    return lax.dot_general(a, b, (((0,), (0,)), ((), ())), preferred_element_type=F32)


def _rms(x, gain):
    return x * lax.rsqrt(jnp.mean(x * x, axis=-1, keepdims=True) + EPS) * gain


def _sigmoid(z):
    return 1.0 / (1.0 + jnp.exp(-z))


def _const_spec(shape):
    nd = len(shape)
    return pl.BlockSpec(shape, lambda *_: (0,) * nd, pipeline_mode=pl.Buffered(1))


def _chunk_cumsum(lf):
    rows = lf.shape[0]
    per = CH // SUBLANES
    y = lf.reshape(rows // SUBLANES, SUBLANES, DK)
    row = lax.broadcasted_iota(jnp.int32, y.shape, 1)
    for s in (1, 2, 4):
        y = y + jnp.where(row >= s, pltpu.roll(y, s, 1), 0.0)
    y = y.reshape(rows // CH, per, SUBLANES, DK)
    tot = jnp.broadcast_to(y[:, :, SUBLANES - 1:SUBLANES, :], y.shape)
    outs = [y[:, 0]]
    carry = tot[:, 0]
    for j in range(1, per):
        outs.append(y[:, j] + carry)
        if j < per - 1:
            carry = carry + tot[:, j]
    return jnp.stack(outs, axis=1).reshape(rows, DK)


def _level_reference(a, b):
    rows = a.shape[0]
    if b >= SUBLANES:
        blk = a.reshape(rows // b, b, DK)
        ref = jnp.broadcast_to(blk[:, b // 2 - 1:b // 2, :], blk.shape)
        return ref.reshape(rows, DK)
    a8 = a.reshape(rows // SUBLANES, SUBLANES, DK)
    row = lax.broadcasted_iota(jnp.int32, a8.shape, 1)
    if b == 4:
        lo = jnp.broadcast_to(a8[:, 1:2, :], a8.shape)
        hi = jnp.broadcast_to(a8[:, 5:6, :], a8.shape)
        ref = jnp.where(row < 4, lo, hi)
    else:
        ref = jnp.where((row & 1) == 0, a8, pltpu.roll(a8, 1, 1))
    return ref.reshape(rows, DK)


def _mix_kernel(x_ref, gain_ref, win_ref, band_ref, poolw_ref, pscale_ref, lbl_ref, hnorm_ref, wout_ref,
                o_ref, proj_ref, mix_ref, tail_ref, state_ref):
    tm = x_ref.shape[0]
    j = pl.program_id(1)

    @pl.when(j == 0)
    def _start_of_sequence():
        tail_ref[...] = jnp.zeros_like(tail_ref)
        state_ref[...] = jnp.zeros_like(state_ref)

    h = _rms(x_ref[...], gain_ref[...]).astype(BF16)
    proj_ref[...] = _dot(h, win_ref[...])

    u = proj_ref[:, 0:POOL_DIM]
    u_ext = jnp.concatenate([tail_ref[...], u], axis=0).astype(BF16)
    tail_ref[...] = u[tm - HALO:, :]
    t_glob = j * tm + lax.broadcasted_iota(jnp.int32, (tm, 1), 0)
    for g, w in enumerate(POOL_WINDOWS):
        sl = slice(g * POOL_GROUP, (g + 1) * POOL_GROUP)
        sums = _dot(band_ref[g], u_ext[:, sl])
        count = jnp.minimum(t_glob + 1, w).astype(F32)
        pooled = sums / count - u[:, sl]
        out = _dot(pooled.astype(BF16), poolw_ref[g]) * pscale_ref[:, sl]
        mix_ref[:, sl] = out.astype(BF16)

    ti = lax.broadcasted_iota(jnp.int32, (CH, CH), 0)
    si = lax.broadcasted_iota(jnp.int32, (CH, CH), 1)
    masks = [((ti ^ si) < b) & ((ti & (b // 2)) != 0) & ((si & (b // 2)) == 0) for b in LEVEL_BLOCKS]
    eye = ti == si
    n_chunks = tm // CH

    def head(hd, carry):
        off = hd * DK
        q_in = proj_ref[:, pl.ds(pl.multiple_of(POOL_DIM + off, DK), DK)]
        f_in = proj_ref[:, pl.ds(pl.multiple_of(POOL_DIM + HGRN_DIM + off, DK), DK)]
        v = proj_ref[:, pl.ds(pl.multiple_of(POOL_DIM + 2 * HGRN_DIM + off, DK), DK)].astype(BF16)
        g_in = proj_ref[:, pl.ds(pl.multiple_of(POOL_DIM + 3 * HGRN_DIM + off, DK), DK)]
        logits = lbl_ref[:, pl.ds(pl.multiple_of(off, DK), DK)]
        l0, l1 = logits[0:1, :], logits[1:2, :]
        lmax = jnp.maximum(l0, l1)
        e0, e1 = jnp.exp(l0 - lmax), jnp.exp(l1 - lmax)
        lb = e0 / (e0 + e1)

        q = q_in * _sigmoid(q_in)
        f = lb + (1.0 - lb) * _sigmoid(f_in)
        k = (1.0 - lb) * _sigmoid(-f_in)
        a = _chunk_cumsum(jnp.log(f))

        scores = [jnp.zeros((CH, CH), F32) for _ in range(n_chunks)]
        for b, mask in zip(LEVEL_BLOCKS, masks):
            e = jnp.exp(-jnp.abs(a - _level_reference(a, b)))
            qe = (q * e).astype(BF16)
            ke = (k * e).astype(BF16)
            for c in range(n_chunks):
                rows = slice(c * CH, (c + 1) * CH)
                scores[c] = scores[c] + jnp.where(mask, _dot_nt(qe[rows], ke[rows]), 0.0)
        qb, kb = q.astype(BF16), k.astype(BF16)
        gate = g_in * _sigmoid(g_in)
        for c in range(n_chunks):
            rows = slice(c * CH, (c + 1) * CH)
            s_c = scores[c] + jnp.where(eye, _dot_nt(qb[rows], kb[rows]), 0.0)
            a_c = a[rows]
            a_last = a_c[CH - 1:CH, :]
            q0 = (q[rows] * jnp.exp(a_c)).astype(BF16)
            k_end = (k[rows] * jnp.exp(a_last - a_c)).astype(BF16)
            st = state_ref[hd]
            o = _dot(s_c.astype(BF16), v[rows]) + _dot_nt(q0, st.astype(BF16))
            state_ref[hd] = st * jnp.exp(a_last) + _dot_tn(v[rows], k_end)
            o = _rms(o, hnorm_ref[...]) * gate[rows]
            mix_ref[rows, pl.ds(pl.multiple_of(POOL_DIM + off, DK), DK)] = o.astype(BF16)
        return carry

    lax.fori_loop(0, HEADS, head, 0)

    o_ref[...] = x_ref[...] + _dot(mix_ref[...], wout_ref[...])


def _pool_bands(tm):
    t = jnp.arange(tm)[:, None]
    c = jnp.arange(HALO + tm)[None, :] - HALO
    return jnp.stack([((c <= t) & (c > t - w)) for w in POOL_WINDOWS]).astype(BF16)


def _mixer(x, norm_mix, w_in, pool_w, pool_scale, lb_logits, hgrn_norm, w_out):
    bsz, seq, _ = x.shape
    tm = TM_MIX
    assert seq % tm == 0 and tm % CH == 0
    return pl.pallas_call(
        _mix_kernel,
        grid=(bsz, seq // tm),
        in_specs=[
            pl.BlockSpec((None, tm, D_MODEL), lambda b, j: (b, j, 0)),
            _const_spec((1, D_MODEL)),
            _const_spec((D_MODEL, IN_DIM)),
            _const_spec((len(POOL_WINDOWS), tm, HALO + tm)),
            _const_spec((len(POOL_WINDOWS), POOL_GROUP, POOL_GROUP)),
            _const_spec((1, POOL_DIM)),
            _const_spec((2, HGRN_DIM)),
            _const_spec((1, DK)),
            _const_spec((D_MODEL, D_MODEL)),
        ],
        out_specs=pl.BlockSpec((None, tm, D_MODEL), lambda b, j: (b, j, 0)),
        out_shape=jax.ShapeDtypeStruct(x.shape, F32),
        scratch_shapes=[
            pltpu.VMEM((tm, IN_DIM), F32),
            pltpu.VMEM((tm, D_MODEL), BF16),
            pltpu.VMEM((HALO, POOL_DIM), F32),
            pltpu.VMEM((HEADS, DK, DK), F32),
        ],
        compiler_params=pltpu.CompilerParams(
            dimension_semantics=("arbitrary", "arbitrary"), vmem_limit_bytes=V7X_VMEM_LIMIT_BYTES),
        name="mixer",
    )(x, norm_mix.reshape(1, D_MODEL), w_in.astype(BF16), _pool_bands(tm), pool_w.astype(BF16),
      pool_scale.reshape(1, POOL_DIM), lb_logits, hgrn_norm.reshape(1, DK), w_out.astype(BF16))


def _kv_kernel(mem_ref, gain_ref, wk_ref, wv_ref, wq_ref, wo_ref, a_ref, b_ref):
    m = _rms(mem_ref[...], gain_ref[...]).astype(BF16)
    k = _dot(m, wk_ref[...]).astype(BF16)
    v = _dot(m, wv_ref[...]).astype(BF16)
    a_ref[...] = (_dot_nt(wq_ref[...], k) * (XHD ** -0.5)).astype(BF16)
    b_ref[...] = _dot(v, wo_ref[...]).astype(BF16)


def _kv_fold(mem, norm_mem, wq, wk, wv, wo):
    bsz, mlen, _ = mem.shape
    return pl.pallas_call(
        _kv_kernel,
        grid=(bsz, XH),
        in_specs=[
            pl.BlockSpec((None, mlen, D_MODEL), lambda b, h: (b, 0, 0)),
            pl.BlockSpec((1, D_MODEL), lambda b, h: (0, 0)),
            pl.BlockSpec((D_MODEL, XHD), lambda b, h: (0, h)),
            pl.BlockSpec((D_MODEL, XHD), lambda b, h: (0, h)),
            pl.BlockSpec((D_MODEL, XHD), lambda b, h: (0, h)),
            pl.BlockSpec((XHD, D_MODEL), lambda b, h: (h, 0)),
        ],
        out_specs=[
            pl.BlockSpec((None, D_MODEL, mlen), lambda b, h: (b, 0, h)),
            pl.BlockSpec((None, mlen, D_MODEL), lambda b, h: (b, h, 0)),
        ],
        out_shape=[
            jax.ShapeDtypeStruct((bsz, D_MODEL, XH * mlen), BF16),
            jax.ShapeDtypeStruct((bsz, XH * mlen, D_MODEL), BF16),
        ],
        compiler_params=pltpu.CompilerParams(
            dimension_semantics=("arbitrary", "arbitrary"), vmem_limit_bytes=V7X_VMEM_LIMIT_BYTES),
        name="kv_fold",
    )(mem, norm_mem.reshape(1, D_MODEL), wk.astype(BF16), wv.astype(BF16), wq.astype(BF16), wo.astype(BF16))


def _first_index_of(values, target):
    idx = jnp.full(target.shape, len(values) - 1, jnp.int32)
    for i in range(len(values) - 2, -1, -1):
        idx = jnp.where(values[i] == target, i, idx)
    return idx


def _xattn_kernel(x_ref, gain_ref, a_ref, b_ref, gain3_ref, wrt_ref, x2_ref, comb_ref):
    mlen = a_ref.shape[1] // XH
    x = x_ref[...]
    h = _rms(x, gain_ref[...]).astype(BF16)
    s = _dot(h, a_ref[...])
    probs = []
    for hd in range(XH):
        sh = s[:, hd * mlen:(hd + 1) * mlen]
        p = jnp.exp(sh - jnp.max(sh, axis=-1, keepdims=True))
        probs.append((p / jnp.sum(p, axis=-1, keepdims=True)).astype(BF16))
    x2 = x + _dot(jnp.concatenate(probs, axis=-1), b_ref[...])
    x2_ref[...] = x2

    h3 = _rms(x2, gain3_ref[...]).astype(BF16)
    lt = _dot_nt(wrt_ref[...], h3)
    gl = [lt[i:i + 1, :] for i in range(N_GROUPS)]
    gmax = functools.reduce(jnp.maximum, gl)
    p_top = 1.0 / sum(jnp.exp(g - gmax) for g in gl)
    gi = _first_index_of(gl, gmax)
    el = [lt[N_GROUPS + i:N_GROUPS + i + 1, :] for i in range(N_EXPERTS)]
    sel = []
    for i in range(EPG):
        v_i = el[(N_GROUPS - 1) * EPG + i]
        for g in range(N_GROUPS - 2, -1, -1):
            v_i = jnp.where(gi == g, el[g * EPG + i], v_i)
        sel.append(v_i)
    v1 = functools.reduce(jnp.maximum, sel)
    i1 = _first_index_of(sel, v1)
    rest = [jnp.where(i1 == i, -jnp.inf, sel[i]) for i in range(EPG)]
    v2 = functools.reduce(jnp.maximum, rest)
    i2 = _first_index_of(rest, v2)
    e2 = jnp.exp(v2 - v1)
    w1 = p_top / (1.0 + e2)
    w2 = p_top * (e2 / (1.0 + e2))
    id1 = gi * EPG + i1
    id2 = gi * EPG + i2
    comb_ref[...] = jnp.concatenate(
        [jnp.where(id1 == e, w1, 0.0) + jnp.where(id2 == e, w2, 0.0) for e in range(N_EXPERTS)], axis=0)


def _xattn(x1, norm_xattn, a_fold, b_fold, norm_ffn, router_group, router_expert, seq):
    tokens = x1.shape[0]
    tm = TM_ATT
    per_batch = seq // tm
    kdim = a_fold.shape[2]
    wrt = jnp.zeros((ROUTER_ROWS, D_MODEL), F32)
    wrt = wrt.at[:N_GROUPS].set(router_group.T).at[N_GROUPS:N_GROUPS + N_EXPERTS].set(router_expert.T)
    return pl.pallas_call(
        _xattn_kernel,
        grid=(tokens // tm,),
        in_specs=[
            pl.BlockSpec((tm, D_MODEL), lambda i: (i, 0)),
            _const_spec((1, D_MODEL)),
            pl.BlockSpec((None, D_MODEL, kdim), lambda i: (i // per_batch, 0, 0)),
            pl.BlockSpec((None, kdim, D_MODEL), lambda i: (i // per_batch, 0, 0)),
            _const_spec((1, D_MODEL)),
            _const_spec((ROUTER_ROWS, D_MODEL)),
        ],
        out_specs=[
            pl.BlockSpec((tm, D_MODEL), lambda i: (i, 0)),
            pl.BlockSpec((N_EXPERTS, tm), lambda i: (0, i)),
        ],
        out_shape=[
            jax.ShapeDtypeStruct((tokens, D_MODEL), F32),
            jax.ShapeDtypeStruct((N_EXPERTS, tokens), F32),
        ],
        compiler_params=pltpu.CompilerParams(
            dimension_semantics=("arbitrary",), vmem_limit_bytes=V7X_VMEM_LIMIT_BYTES),
        name="xattn_router",
    )(x1, norm_xattn.reshape(1, D_MODEL), a_fold, b_fold, norm_ffn.reshape(1, D_MODEL), wrt.astype(BF16))


def _moe_kernel(x_ref, comb_ref, gain_ref, wg_ref, wu_ref, wd_ref, gfin_ref, o_ref, h_ref, acc_ref):
    e = pl.program_id(1)

    @pl.when(e == 0)
    def _first_expert():
        h_ref[...] = _rms(x_ref[...], gain_ref[...]).astype(BF16)
        acc_ref[...] = jnp.zeros_like(acc_ref)

    h = h_ref[...]
    gate = _dot(h, wg_ref[...])
    hid = gate * _sigmoid(gate) * _dot(h, wu_ref[...])
    lane = lax.broadcasted_iota(jnp.int32, comb_ref.shape, 1)
    c_e = jnp.sum(jnp.where(lane == e, comb_ref[...], 0.0), axis=-1, keepdims=True)
    acc_ref[...] += _dot((hid * c_e).astype(BF16), wd_ref[...])

    @pl.when(e == N_EXPERTS - 1)
    def _last_expert():
        o_ref[...] = _rms(x_ref[...] + acc_ref[...], gfin_ref[...])


def _moe(x2, comb, norm_ffn, w_gate, w_up, w_down, norm_final):
    tokens = x2.shape[0]
    tm = TM_MOE
    return pl.pallas_call(
        _moe_kernel,
        grid=(tokens // tm, N_EXPERTS),
        in_specs=[
            pl.BlockSpec((tm, D_MODEL), lambda i, e: (i, 0)),
            pl.BlockSpec((tm, N_EXPERTS), lambda i, e: (i, 0)),
            _const_spec((1, D_MODEL)),
            pl.BlockSpec((None, D_MODEL, D_EXPERT), lambda i, e: (e, 0, 0)),
            pl.BlockSpec((None, D_MODEL, D_EXPERT), lambda i, e: (e, 0, 0)),
            pl.BlockSpec((None, D_EXPERT, D_MODEL), lambda i, e: (e, 0, 0)),
            _const_spec((1, D_MODEL)),
        ],
        out_specs=pl.BlockSpec((tm, D_MODEL), lambda i, e: (i, 0)),
        out_shape=jax.ShapeDtypeStruct((tokens, D_MODEL), F32),
        scratch_shapes=[pltpu.VMEM((tm, D_MODEL), BF16), pltpu.VMEM((tm, D_MODEL), F32)],
        compiler_params=pltpu.CompilerParams(
            dimension_semantics=("arbitrary", "arbitrary"), vmem_limit_bytes=V7X_VMEM_LIMIT_BYTES),
        name="moe_dense",
    )(x2, comb, norm_ffn.reshape(1, D_MODEL), w_gate.astype(BF16), w_up.astype(BF16), w_down.astype(BF16),
      norm_final.reshape(1, D_MODEL))


def kernel(x, mem, norm_mix, w_in, pool_w, pool_scale, hgrn_lb_logits, hgrn_norm, w_out, norm_xattn, norm_mem,
           xattn_wq, xattn_wk, xattn_wv, xattn_wo, norm_ffn, router_group, router_expert, w_gate, w_up, w_down,
           norm_final):
    bsz, seq, _ = x.shape
    assert norm_mix.shape[0] == 1, "one layer"
    x1 = _mixer(x, norm_mix[0], w_in[0], pool_w[0], pool_scale[0], hgrn_lb_logits, hgrn_norm[0], w_out[0])
    a_fold, b_fold = _kv_fold(mem, norm_mem[0], xattn_wq[0], xattn_wk[0], xattn_wv[0], xattn_wo[0])
    x2, comb_t = _xattn(x1.reshape(bsz * seq, D_MODEL), norm_xattn[0], a_fold, b_fold, norm_ffn[0],
                        router_group[0], router_expert[0], seq)
    out = _moe(x2, comb_t.T, norm_ffn[0], w_gate[0], w_up[0], w_down[0], norm_final)
    return out.reshape(bsz, seq, D_MODEL)
```

```python
import functools

import jax
import jax.numpy as jnp
from jax import lax
from jax.experimental import pallas as pl
from jax.experimental.pallas import tpu as pltpu

F32 = jnp.float32
BF16 = jnp.bfloat16

D_MODEL = 2048
EPS = 1e-6
POOL_WINDOWS = (2, 4, 8, 16)
POOL_GROUP = 256
POOL_DIM = POOL_GROUP * len(POOL_WINDOWS)
HEADS = 8
DK = 128
HGRN_DIM = HEADS * DK
IN_DIM = POOL_DIM + 4 * HGRN_DIM
XH = 4
XHD = D_MODEL // XH
N_GROUPS = 4
EPG = 4
N_EXPERTS = N_GROUPS * EPG
D_EXPERT = D_MODEL // 4

V7X_VMEM_LIMIT_BYTES = 56 * 2**20
SUBLANES = 8
LANES = 128
TM_MIX = 256
CH = 128
HALO = max(POOL_WINDOWS)
LEVEL_BLOCKS = (128, 64, 32, 16, 8, 4, 2)
TM_ATT = 256
ROUTER_ROWS = 32
PAIRS_PER_GROUP = EPG * (EPG - 1) // 2
N_CLASSES = N_GROUPS * PAIRS_PER_GROUP
CLASS_ROWS = 32
PAIR_LO = (0, 0, 0, 1, 1, 2)
PAIR_HI = (1, 2, 3, 2, 3, 3)
TM_MOE = 256
TT_RANK = 512


def _dot(a, b):
    return jnp.dot(a, b, preferred_element_type=F32)


def _dot_nt(a, b):
    return lax.dot_general(a, b, (((1,), (1,)), ((), ())), preferred_element_type=F32)


def _dot_tn(a, b):
    return lax.dot_general(a, b, (((0,), (0,)), ((), ())), preferred_element_type=F32)


def _rms(x, gain):
    return x * lax.rsqrt(jnp.mean(x * x, axis=-1, keepdims=True) + EPS) * gain


def _sigmoid(z):
    return 1.0 / (1.0 + jnp.exp(-z))


def _const_spec(shape):
    nd = len(shape)
    return pl.BlockSpec(shape, lambda *_: (0,) * nd, pipeline_mode=pl.Buffered(1))


def _chunk_cumsum(lf):
    rows = lf.shape[0]
    per = CH // SUBLANES
    y = lf.reshape(rows // SUBLANES, SUBLANES, DK)
    row = lax.broadcasted_iota(jnp.int32, y.shape, 1)
    for s in (1, 2, 4):
        y = y + jnp.where(row >= s, pltpu.roll(y, s, 1), 0.0)
    y = y.reshape(rows // CH, per, SUBLANES, DK)
    tot = jnp.broadcast_to(y[:, :, SUBLANES - 1:SUBLANES, :], y.shape)
    outs = [y[:, 0]]
    carry = tot[:, 0]
    for j in range(1, per):
        outs.append(y[:, j] + carry)
        if j < per - 1:
            carry = carry + tot[:, j]
    return jnp.stack(outs, axis=1).reshape(rows, DK)


def _level_reference(a, b):
    rows = a.shape[0]
    if b >= SUBLANES:
        blk = a.reshape(rows // b, b, DK)
        ref = jnp.broadcast_to(blk[:, b // 2 - 1:b // 2, :], blk.shape)
        return ref.reshape(rows, DK)
    a8 = a.reshape(rows // SUBLANES, SUBLANES, DK)
    row = lax.broadcasted_iota(jnp.int32, a8.shape, 1)
    if b == 4:
        lo = jnp.broadcast_to(a8[:, 1:2, :], a8.shape)
        hi = jnp.broadcast_to(a8[:, 5:6, :], a8.shape)
        ref = jnp.where(row < 4, lo, hi)
    else:
        ref = jnp.where((row & 1) == 0, a8, pltpu.roll(a8, 1, 1))
    return ref.reshape(rows, DK)


def _mix_kernel(x_ref, gain_ref, win_ref, band_ref, poolw_ref, pscale_ref, lbl_ref, hnorm_ref, wout_ref,
                o_ref, proj_ref, mix_ref, tail_ref, state_ref):
    tm = x_ref.shape[0]
    j = pl.program_id(1)

    @pl.when(j == 0)
    def _start_of_sequence():
        tail_ref[...] = jnp.zeros_like(tail_ref)
        state_ref[...] = jnp.zeros_like(state_ref)

    h = _rms(x_ref[...], gain_ref[...]).astype(BF16)
    proj_ref[...] = _dot(h, win_ref[...])

    u = proj_ref[:, 0:POOL_DIM]
    u_ext = jnp.concatenate([tail_ref[...], u], axis=0).astype(BF16)
    tail_ref[...] = u[tm - HALO:, :]
    t_glob = j * tm + lax.broadcasted_iota(jnp.int32, (tm, 1), 0)
    for g, w in enumerate(POOL_WINDOWS):
        sl = slice(g * POOL_GROUP, (g + 1) * POOL_GROUP)
        sums = _dot(band_ref[g], u_ext[:, sl])
        count = jnp.minimum(t_glob + 1, w).astype(F32)
        pooled = sums / count - u[:, sl]
        out = _dot(pooled.astype(BF16), poolw_ref[g]) * pscale_ref[:, sl]
        mix_ref[:, sl] = out.astype(BF16)

    ti = lax.broadcasted_iota(jnp.int32, (CH, CH), 0)
    si = lax.broadcasted_iota(jnp.int32, (CH, CH), 1)
    masks = [((ti ^ si) < b) & ((ti & (b // 2)) != 0) & ((si & (b // 2)) == 0) for b in LEVEL_BLOCKS]
    eye = ti == si
    n_chunks = tm // CH

    def head(hd, carry):
        off = hd * DK
        q_in = proj_ref[:, pl.ds(pl.multiple_of(POOL_DIM + off, DK), DK)]
        f_in = proj_ref[:, pl.ds(pl.multiple_of(POOL_DIM + HGRN_DIM + off, DK), DK)]
        v = proj_ref[:, pl.ds(pl.multiple_of(POOL_DIM + 2 * HGRN_DIM + off, DK), DK)].astype(BF16)
        g_in = proj_ref[:, pl.ds(pl.multiple_of(POOL_DIM + 3 * HGRN_DIM + off, DK), DK)]
        logits = lbl_ref[:, pl.ds(pl.multiple_of(off, DK), DK)]
        l0, l1 = logits[0:1, :], logits[1:2, :]
        lmax = jnp.maximum(l0, l1)
        e0, e1 = jnp.exp(l0 - lmax), jnp.exp(l1 - lmax)
        lb = e0 / (e0 + e1)

        q = q_in * _sigmoid(q_in)
        f = lb + (1.0 - lb) * _sigmoid(f_in)
        k = (1.0 - lb) * _sigmoid(-f_in)
        a = _chunk_cumsum(jnp.log(f))

        scores = [jnp.zeros((CH, CH), F32) for _ in range(n_chunks)]
        for b, mask in zip(LEVEL_BLOCKS, masks):
            e = jnp.exp(-jnp.abs(a - _level_reference(a, b)))
            qe = (q * e).astype(BF16)
            ke = (k * e).astype(BF16)
            for c in range(n_chunks):
                rows = slice(c * CH, (c + 1) * CH)
                scores[c] = scores[c] + jnp.where(mask, _dot_nt(qe[rows], ke[rows]), 0.0)
        qb, kb = q.astype(BF16), k.astype(BF16)
        gate = g_in * _sigmoid(g_in)
        for c in range(n_chunks):
            rows = slice(c * CH, (c + 1) * CH)
            s_c = scores[c] + jnp.where(eye, _dot_nt(qb[rows], kb[rows]), 0.0)
            a_c = a[rows]
            a_last = a_c[CH - 1:CH, :]
            q0 = (q[rows] * jnp.exp(a_c)).astype(BF16)
            k_end = (k[rows] * jnp.exp(a_last - a_c)).astype(BF16)
            st = state_ref[hd]
            o = _dot(s_c.astype(BF16), v[rows]) + _dot_nt(q0, st.astype(BF16))
            state_ref[hd] = st * jnp.exp(a_last) + _dot_tn(v[rows], k_end)
            o = _rms(o, hnorm_ref[...]) * gate[rows]
            mix_ref[rows, pl.ds(pl.multiple_of(POOL_DIM + off, DK), DK)] = o.astype(BF16)
        return carry

    lax.fori_loop(0, HEADS, head, 0)

    o_ref[...] = x_ref[...] + _dot(mix_ref[...], wout_ref[...])


def _pool_bands(tm):
    t = jnp.arange(tm)[:, None]
    c = jnp.arange(HALO + tm)[None, :] - HALO
    return jnp.stack([((c <= t) & (c > t - w)) for w in POOL_WINDOWS]).astype(BF16)


def _mixer(x, norm_mix, w_in, pool_w, pool_scale, lb_logits, hgrn_norm, w_out):
    bsz, seq, _ = x.shape
    tm = TM_MIX
    assert seq % tm == 0 and tm % CH == 0
    return pl.pallas_call(
        _mix_kernel,
        grid=(bsz, seq // tm),
        in_specs=[
            pl.BlockSpec((None, tm, D_MODEL), lambda b, j: (b, j, 0)),
            _const_spec((1, D_MODEL)),
            _const_spec((D_MODEL, IN_DIM)),
            _const_spec((len(POOL_WINDOWS), tm, HALO + tm)),
            _const_spec((len(POOL_WINDOWS), POOL_GROUP, POOL_GROUP)),
            _const_spec((1, POOL_DIM)),
            _const_spec((2, HGRN_DIM)),
            _const_spec((1, DK)),
            _const_spec((D_MODEL, D_MODEL)),
        ],
        out_specs=pl.BlockSpec((None, tm, D_MODEL), lambda b, j: (b, j, 0)),
        out_shape=jax.ShapeDtypeStruct(x.shape, F32),
        scratch_shapes=[
            pltpu.VMEM((tm, IN_DIM), F32),
            pltpu.VMEM((tm, D_MODEL), BF16),
            pltpu.VMEM((HALO, POOL_DIM), F32),
            pltpu.VMEM((HEADS, DK, DK), F32),
        ],
        compiler_params=pltpu.CompilerParams(
            dimension_semantics=("arbitrary", "arbitrary"), vmem_limit_bytes=V7X_VMEM_LIMIT_BYTES),
        name="mixer",
    )(x, norm_mix.reshape(1, D_MODEL), w_in.astype(BF16), _pool_bands(tm), pool_w.astype(BF16),
      pool_scale.reshape(1, POOL_DIM), lb_logits, hgrn_norm.reshape(1, DK), w_out.astype(BF16))


def _kv_kernel(mem_ref, gain_ref, wk_ref, wv_ref, wq_ref, wo_ref, a_ref, b_ref):
    m = _rms(mem_ref[...], gain_ref[...]).astype(BF16)
    k = _dot(m, wk_ref[...]).astype(BF16)
    v = _dot(m, wv_ref[...]).astype(BF16)
    a_ref[...] = (_dot_nt(wq_ref[...], k) * (XHD ** -0.5)).astype(BF16)
    b_ref[...] = _dot(v, wo_ref[...]).astype(BF16)


def _kv_fold(mem, norm_mem, wq, wk, wv, wo):
    bsz, mlen, _ = mem.shape
    return pl.pallas_call(
        _kv_kernel,
        grid=(bsz, XH),
        in_specs=[
            pl.BlockSpec((None, mlen, D_MODEL), lambda b, h: (b, 0, 0)),
            pl.BlockSpec((1, D_MODEL), lambda b, h: (0, 0)),
            pl.BlockSpec((D_MODEL, XHD), lambda b, h: (0, h)),
            pl.BlockSpec((D_MODEL, XHD), lambda b, h: (0, h)),
            pl.BlockSpec((D_MODEL, XHD), lambda b, h: (0, h)),
            pl.BlockSpec((XHD, D_MODEL), lambda b, h: (h, 0)),
        ],
        out_specs=[
            pl.BlockSpec((None, D_MODEL, mlen), lambda b, h: (b, 0, h)),
            pl.BlockSpec((None, mlen, D_MODEL), lambda b, h: (b, h, 0)),
        ],
        out_shape=[
            jax.ShapeDtypeStruct((bsz, D_MODEL, XH * mlen), BF16),
            jax.ShapeDtypeStruct((bsz, XH * mlen, D_MODEL), BF16),
        ],
        compiler_params=pltpu.CompilerParams(
            dimension_semantics=("arbitrary", "arbitrary"), vmem_limit_bytes=V7X_VMEM_LIMIT_BYTES),
        name="kv_fold",
    )(mem, norm_mem.reshape(1, D_MODEL), wk.astype(BF16), wv.astype(BF16), wq.astype(BF16), wo.astype(BF16))


def _first_index_of(values, target):
    idx = jnp.full(target.shape, len(values) - 1, jnp.int32)
    for i in range(len(values) - 2, -1, -1):
        idx = jnp.where(values[i] == target, i, idx)
    return idx


def _xattn_kernel(x_ref, gain_ref, a_ref, b_ref, gain3_ref, wrt_ref, x2_ref, cls_ref, cnt_ref):
    mlen = a_ref.shape[1] // XH

    @pl.when(pl.program_id(0) == 0)
    def _first_tile():
        cnt_ref[...] = jnp.zeros_like(cnt_ref)

    x = x_ref[...]
    h = _rms(x, gain_ref[...]).astype(BF16)
    s = _dot(h, a_ref[...])
    probs = []
    for hd in range(XH):
        sh = s[:, hd * mlen:(hd + 1) * mlen]
        p = jnp.exp(sh - jnp.max(sh, axis=-1, keepdims=True))
        probs.append((p / jnp.sum(p, axis=-1, keepdims=True)).astype(BF16))
    x2 = x + _dot(jnp.concatenate(probs, axis=-1), b_ref[...])
    x2_ref[...] = x2

    h3 = _rms(x2, gain3_ref[...]).astype(BF16)
    lt = _dot_nt(wrt_ref[...], h3)
    gl = [lt[i:i + 1, :] for i in range(N_GROUPS)]
    gmax = functools.reduce(jnp.maximum, gl)
    gi = _first_index_of(gl, gmax)
    el = [lt[N_GROUPS + i:N_GROUPS + i + 1, :] for i in range(N_EXPERTS)]
    sel = []
    for i in range(EPG):
        v_i = el[(N_GROUPS - 1) * EPG + i]
        for g in range(N_GROUPS - 2, -1, -1):
            v_i = jnp.where(gi == g, el[g * EPG + i], v_i)
        sel.append(v_i)
    v1 = functools.reduce(jnp.maximum, sel)
    i1 = _first_index_of(sel, v1)
    rest = [jnp.where(i1 == i, -jnp.inf, sel[i]) for i in range(EPG)]
    v2 = functools.reduce(jnp.maximum, rest)
    i2 = _first_index_of(rest, v2)
    lo, hi = jnp.minimum(i1, i2), jnp.maximum(i1, i2)
    pair = jnp.where(lo == 0, hi - 1, jnp.where(lo == 1, hi + 1, PAIRS_PER_GROUP - 1))
    cls = gi * PAIRS_PER_GROUP + pair
    cls_ref[...] = jnp.broadcast_to(cls, cls_ref.shape)
    cid = lax.broadcasted_iota(jnp.int32, (CLASS_ROWS, cls.shape[1]), 0)
    per_class = jnp.sum((cid == cls).astype(F32), axis=-1, keepdims=True)
    cnt_ref[...] += jnp.broadcast_to(per_class, cnt_ref.shape)


def _xattn(x1, norm_xattn, a_fold, b_fold, norm_ffn, router_group, router_expert, seq):
    tokens = x1.shape[0]
    tm = TM_ATT
    per_batch = seq // tm
    kdim = a_fold.shape[2]
    wrt = jnp.zeros((ROUTER_ROWS, D_MODEL), F32)
    wrt = wrt.at[:N_GROUPS].set(router_group.T).at[N_GROUPS:N_GROUPS + N_EXPERTS].set(router_expert.T)
    return pl.pallas_call(
        _xattn_kernel,
        grid=(tokens // tm,),
        in_specs=[
            pl.BlockSpec((tm, D_MODEL), lambda i: (i, 0)),
            _const_spec((1, D_MODEL)),
            pl.BlockSpec((None, D_MODEL, kdim), lambda i: (i // per_batch, 0, 0)),
            pl.BlockSpec((None, kdim, D_MODEL), lambda i: (i // per_batch, 0, 0)),
            _const_spec((1, D_MODEL)),
            _const_spec((ROUTER_ROWS, D_MODEL)),
        ],
        out_specs=[
            pl.BlockSpec((tm, D_MODEL), lambda i: (i, 0)),
            pl.BlockSpec((SUBLANES, tm), lambda i: (0, i)),
            pl.BlockSpec((CLASS_ROWS, LANES), lambda i: (0, 0)),
        ],
        out_shape=[
            jax.ShapeDtypeStruct((tokens, D_MODEL), F32),
            jax.ShapeDtypeStruct((SUBLANES, tokens), jnp.int32),
            jax.ShapeDtypeStruct((CLASS_ROWS, LANES), F32),
        ],
        compiler_params=pltpu.CompilerParams(
            dimension_semantics=("arbitrary",), vmem_limit_bytes=V7X_VMEM_LIMIT_BYTES),
        name="xattn_router",
    )(x1, norm_xattn.reshape(1, D_MODEL), a_fold, b_fold, norm_ffn.reshape(1, D_MODEL), wrt.astype(BF16))


def _rank_kernel(cls_ref, off_ref, tri_ref, pos_ref, run_ref):
    @pl.when(pl.program_id(0) == 0)
    def _first_tile():
        run_ref[...] = jnp.zeros_like(run_ref)

    cls = cls_ref[0:1, :]
    tt = cls.shape[1]
    cid = lax.broadcasted_iota(jnp.int32, (CLASS_ROWS, tt), 0)
    onehot = cid == cls
    cum = _dot(onehot.astype(BF16), tri_ref[...])
    base = off_ref[:, 0:1] + run_ref[:, 0:1]
    pos = jnp.sum(jnp.where(onehot, cum - 1.0 + base, 0.0), axis=0, keepdims=True)
    pos_ref[...] = jnp.broadcast_to(pos.astype(jnp.int32), pos_ref.shape)
    run_ref[...] += jnp.broadcast_to(cum[:, tt - 1:tt], run_ref.shape)


def _rank(cls, offsets):
    tokens = cls.shape[1]
    tt = TT_RANK
    tri = (jnp.arange(tt)[:, None] <= jnp.arange(tt)[None, :]).astype(BF16)
    off = jnp.broadcast_to(offsets.astype(F32)[:, None], (CLASS_ROWS, LANES))
    return pl.pallas_call(
        _rank_kernel,
        grid=(tokens // tt,),
        in_specs=[
            pl.BlockSpec((SUBLANES, tt), lambda i: (0, i)),
            _const_spec((CLASS_ROWS, LANES)),
            _const_spec((tt, tt)),
        ],
        out_specs=pl.BlockSpec((SUBLANES, tt), lambda i: (0, i)),
        out_shape=jax.ShapeDtypeStruct((SUBLANES, tokens), jnp.int32),
        scratch_shapes=[pltpu.VMEM((CLASS_ROWS, LANES), F32)],
        compiler_params=pltpu.CompilerParams(dimension_semantics=("arbitrary",)),
        name="rank",
    )(cls, off, tri)


def _moe_kernel(elo_ref, ehi_ref, grp_ref, nvalid_ref, nused_ref,
                inv_ref, invn_ref, x_hbm, gain_ref, wr_ref, wg_lo, wu_lo, wd_lo, wg_hi, wu_hi, wd_hi, gfin_ref,
                out_hbm, xbuf, obuf, gsem, ssem):
    k = pl.program_id(0)
    n_used = nused_ref[0]
    tm = obuf.shape[0]
    slot = k % 2

    def start_gather(idx_ref, dst_slot):
        def one_row(r, c):
            pltpu.make_async_copy(x_hbm.at[pl.ds(idx_ref[0, r], 1)], xbuf.at[dst_slot, pl.ds(r, 1)],
                                  gsem.at[dst_slot]).start()
            return c
        lax.fori_loop(0, tm, one_row, 0, unroll=8)

    def wait_gather(dst_slot):
        def one_row(r, c):
            pltpu.make_async_copy(x_hbm.at[pl.ds(0, 1)], xbuf.at[dst_slot, pl.ds(0, 1)], gsem.at[dst_slot]).wait()
            return c
        lax.fori_loop(0, tm, one_row, 0, unroll=True)

    def start_scatter(n_rows):
        def one_row(r, c):
            pltpu.make_async_copy(obuf.at[pl.ds(r, 1)], out_hbm.at[pl.ds(inv_ref[0, r], 1)], ssem.at[0]).start()
            return c
        lax.fori_loop(0, n_rows, one_row, 0)

    def wait_scatter(n_rows):
        def one_row(r, c):
            pltpu.make_async_copy(obuf.at[pl.ds(0, 1)], out_hbm.at[pl.ds(0, 1)], ssem.at[0]).wait()
            return c
        lax.fori_loop(0, n_rows, one_row, 0)

    @pl.when(k < n_used)
    def _tile():
        @pl.when(k == 0)
        def _prologue():
            start_gather(inv_ref, 0)

        @pl.when(k + 1 < n_used)
        def _prefetch():
            start_gather(invn_ref, 1 - slot)

        wait_gather(slot)
        x = xbuf[slot]
        h = _rms(x, gain_ref[...]).astype(BF16)

        logits = _dot(h, wr_ref[...])
        lane = lax.broadcasted_iota(jnp.int32, logits.shape, 1)
        is_group = lane < N_GROUPS
        gmax = jnp.max(jnp.where(is_group, logits, -jnp.inf), axis=-1, keepdims=True)
        denom = jnp.sum(jnp.where(is_group, jnp.exp(logits - gmax), 0.0), axis=-1, keepdims=True)

        def pick(col):
            return jnp.sum(jnp.where(lane == col, logits, 0.0), axis=-1, keepdims=True)

        p_group = jnp.exp(pick(grp_ref[k]) - gmax) / denom
        v_lo = pick(N_GROUPS + elo_ref[k])
        v_hi = pick(N_GROUPS + ehi_ref[k])
        vmax = jnp.maximum(v_lo, v_hi)
        e_lo, e_hi = jnp.exp(v_lo - vmax), jnp.exp(v_hi - vmax)
        c_lo = p_group * (e_lo / (e_lo + e_hi))
        c_hi = p_group * (e_hi / (e_lo + e_hi))

        def expert(wg, wu, wd, c):
            gate = _dot(h, wg[...])
            hid = gate * _sigmoid(gate) * _dot(h, wu[...])
            return _dot((hid * c).astype(BF16), wd[...])

        y = expert(wg_lo, wu_lo, wd_lo, c_lo) + expert(wg_hi, wu_hi, wd_hi, c_hi)
        out = _rms(x + y, gfin_ref[...])

        @pl.when(k > 0)
        def _previous_scatter_done():
            wait_scatter(nvalid_ref[k - 1])

        obuf[...] = out
        start_scatter(nvalid_ref[k])

        @pl.when(k == n_used - 1)
        def _epilogue():
            wait_scatter(nvalid_ref[k])


def _moe(x2, cls, counts, norm_ffn, router_group, router_expert, w_gate, w_up, w_down, norm_final):
    tokens = x2.shape[0]
    tm = TM_MOE
    n_tiles = tokens // tm + N_CLASSES

    counts = counts[:N_CLASSES, 0].astype(jnp.int32)
    tiles_per_class = (counts + tm - 1) // tm
    tile_end = jnp.cumsum(tiles_per_class)
    tile_start = tile_end - tiles_per_class
    n_used = tile_end[-1]
    tile_ids = jnp.arange(n_tiles, dtype=jnp.int32)
    last_used = jnp.minimum(tile_ids, n_used - 1)
    tile_cls = jnp.sum((tile_end[None, :] <= last_used[:, None]).astype(jnp.int32), axis=1)
    tile_cls = jnp.minimum(tile_cls, N_CLASSES - 1)
    n_valid = jnp.clip(counts[tile_cls] - (tile_ids - tile_start[tile_cls]) * tm, 0, tm)
    n_valid = jnp.where(tile_ids < n_used, n_valid, 0).astype(jnp.int32)
    grp = tile_cls // PAIRS_PER_GROUP
    pair = tile_cls % PAIRS_PER_GROUP
    e_lo = (grp * EPG + jnp.asarray(PAIR_LO, jnp.int32)[pair]).astype(jnp.int32)
    e_hi = (grp * EPG + jnp.asarray(PAIR_HI, jnp.int32)[pair]).astype(jnp.int32)
    offsets = jnp.zeros((CLASS_ROWS,), jnp.int32).at[:N_CLASSES].set(tile_start * tm)

    pos = _rank(cls, offsets)[0]
    inv = jnp.zeros((n_tiles * tm,), jnp.int32).at[pos].set(jnp.arange(tokens, dtype=jnp.int32))
    inv = inv.reshape(n_tiles, 1, tm)

    wr = jnp.zeros((D_MODEL, LANES), F32)
    wr = wr.at[:, :N_GROUPS].set(router_group).at[:, N_GROUPS:N_GROUPS + N_EXPERTS].set(router_expert)
    wg, wu, wd = w_gate.astype(BF16), w_up.astype(BF16), w_down.astype(BF16)

    def lo_block(k, elo, ehi, g, nv, nu):
        return (elo[k], 0, 0)

    def hi_block(k, elo, ehi, g, nv, nu):
        return (ehi[k], 0, 0)

    def const2(k, *_):
        return (0, 0)

    up_spec = functools.partial(pl.BlockSpec, (None, D_MODEL, D_EXPERT))
    down_spec = functools.partial(pl.BlockSpec, (None, D_EXPERT, D_MODEL))
    grid_spec = pltpu.PrefetchScalarGridSpec(
        num_scalar_prefetch=5,
        grid=(n_tiles,),
        in_specs=[
            pl.BlockSpec((None, 1, tm), lambda k, *_: (k, 0, 0), memory_space=pltpu.SMEM),
            pl.BlockSpec((None, 1, tm), lambda k, *_: (jnp.minimum(k + 1, n_tiles - 1), 0, 0),
                         memory_space=pltpu.SMEM),
            pl.BlockSpec(memory_space=pl.ANY),
            pl.BlockSpec((1, D_MODEL), const2),
            pl.BlockSpec((D_MODEL, LANES), const2),
            up_spec(lo_block), up_spec(lo_block), down_spec(lo_block),
            up_spec(hi_block), up_spec(hi_block), down_spec(hi_block),
            pl.BlockSpec((1, D_MODEL), const2),
        ],
        out_specs=pl.BlockSpec(memory_space=pl.ANY),
        scratch_shapes=[
            pltpu.VMEM((2, tm, D_MODEL), F32),
            pltpu.VMEM((tm, D_MODEL), F32),
            pltpu.SemaphoreType.DMA((2,)),
            pltpu.SemaphoreType.DMA((1,)),
        ],
    )
    return pl.pallas_call(
        _moe_kernel,
        grid_spec=grid_spec,
        out_shape=jax.ShapeDtypeStruct((tokens, D_MODEL), F32),
        compiler_params=pltpu.CompilerParams(
            dimension_semantics=("arbitrary",), vmem_limit_bytes=V7X_VMEM_LIMIT_BYTES),
        name="moe_sorted",
    )(e_lo, e_hi, grp.astype(jnp.int32), n_valid, n_used.reshape(1).astype(jnp.int32),
      inv, inv, x2, norm_ffn.reshape(1, D_MODEL), wr.astype(BF16), wg, wu, wd, wg, wu, wd,
      norm_final.reshape(1, D_MODEL))


def kernel(x, mem, norm_mix, w_in, pool_w, pool_scale, hgrn_lb_logits, hgrn_norm, w_out, norm_xattn, norm_mem,
           xattn_wq, xattn_wk, xattn_wv, xattn_wo, norm_ffn, router_group, router_expert, w_gate, w_up, w_down,
           norm_final):
    bsz, seq, _ = x.shape
    assert norm_mix.shape[0] == 1, "one layer"
    x1 = _mixer(x, norm_mix[0], w_in[0], pool_w[0], pool_scale[0], hgrn_lb_logits, hgrn_norm[0], w_out[0])
    a_fold, b_fold = _kv_fold(mem, norm_mem[0], xattn_wq[0], xattn_wk[0], xattn_wv[0], xattn_wo[0])
    x2, cls, counts = _xattn(x1.reshape(bsz * seq, D_MODEL), norm_xattn[0], a_fold, b_fold, norm_ffn[0],
                             router_group[0], router_expert[0], seq)
    out = _moe(x2, cls, counts, norm_ffn[0], router_group[0], router_expert[0], w_gate[0], w_up[0], w_down[0],
               norm_final)
    return out.reshape(bsz, seq, D_MODEL)
```

```python
import functools

import jax
import jax.numpy as jnp
from jax import lax
from jax.experimental import pallas as pl
from jax.experimental.pallas import tpu as pltpu

F32 = jnp.float32
BF16 = jnp.bfloat16

D_MODEL = 2048
EPS = 1e-6
POOL_WINDOWS = (2, 4, 8, 16)
POOL_GROUP = 256
POOL_DIM = POOL_GROUP * len(POOL_WINDOWS)
HEADS = 8
DK = 128
HGRN_DIM = HEADS * DK
IN_DIM = POOL_DIM + 4 * HGRN_DIM
HEAD_COLS = 4 * DK
XH = 4
XHD = D_MODEL // XH
N_GROUPS = 4
EPG = 4
N_EXPERTS = N_GROUPS * EPG
D_EXPERT = D_MODEL // 4

V7X_VMEM_LIMIT_BYTES = 56 * 2**20
SUBLANES = 8
LANES = 128
TM_MIX = 256
CH = 128
HALO = max(POOL_WINDOWS)
LEVEL_BLOCKS = (128, 64, 32, 16, 8, 4, 2)
TM_ATT = 256
ROUTER_ROWS = 32
PAIRS_PER_GROUP = EPG * (EPG - 1) // 2
N_CLASSES = N_GROUPS * PAIRS_PER_GROUP
CLASS_ROWS = 32
PAIR_LO = (0, 0, 0, 1, 1, 2)
PAIR_HI = (1, 2, 3, 2, 3, 3)
TM_MOE = 256
TT_RANK = 512


def _dot(a, b):
    return jnp.dot(a, b, preferred_element_type=F32)


def _dot_nt(a, b):
    return lax.dot_general(a, b, (((1,), (1,)), ((), ())), preferred_element_type=F32)


def _dot_tn(a, b):
    return lax.dot_general(a, b, (((0,), (0,)), ((), ())), preferred_element_type=F32)


def _rms(x, gain):
    return x * lax.rsqrt(jnp.mean(x * x, axis=-1, keepdims=True) + EPS) * gain


def _sigmoid(z):
    return 1.0 / (1.0 + jnp.exp(-z))


def _const_spec(shape):
    nd = len(shape)
    return pl.BlockSpec(shape, lambda *_: (0,) * nd, pipeline_mode=pl.Buffered(1))


def _chunk_cumsum(lf):
    rows = lf.shape[0]
    per = CH // SUBLANES
    y = lf.reshape(rows // SUBLANES, SUBLANES, DK)
    row = lax.broadcasted_iota(jnp.int32, y.shape, 1)
    for s in (1, 2, 4):
        y = y + jnp.where(row >= s, pltpu.roll(y, s, 1), 0.0)
    y = y.reshape(rows // CH, per, SUBLANES, DK)
    tot = jnp.broadcast_to(y[:, :, SUBLANES - 1:SUBLANES, :], y.shape)
    outs = [y[:, 0]]
    carry = tot[:, 0]
    for j in range(1, per):
        outs.append(y[:, j] + carry)
        if j < per - 1:
            carry = carry + tot[:, j]
    return jnp.stack(outs, axis=1).reshape(rows, DK)


def _level_reference(a, b):
    rows = a.shape[0]
    if b >= SUBLANES:
        blk = a.reshape(rows // b, b, DK)
        ref = jnp.broadcast_to(blk[:, b // 2 - 1:b // 2, :], blk.shape)
        return ref.reshape(rows, DK)
    a8 = a.reshape(rows // SUBLANES, SUBLANES, DK)
    row = lax.broadcasted_iota(jnp.int32, a8.shape, 1)
    if b == 4:
        lo = jnp.broadcast_to(a8[:, 1:2, :], a8.shape)
        hi = jnp.broadcast_to(a8[:, 5:6, :], a8.shape)
        ref = jnp.where(row < 4, lo, hi)
    else:
        ref = jnp.where((row & 1) == 0, a8, pltpu.roll(a8, 1, 1))
    return ref.reshape(rows, DK)


def _hgrn_head(blk, logits, hnorm, state_ref, hd, masks, eye):
    tm = blk.shape[0]
    q_in, f_in = blk[:, 0:DK], blk[:, DK:2 * DK]
    v = blk[:, 2 * DK:3 * DK].astype(BF16)
    g_in = blk[:, 3 * DK:4 * DK]
    l0, l1 = logits[0:1, :], logits[1:2, :]
    lmax = jnp.maximum(l0, l1)
    e0, e1 = jnp.exp(l0 - lmax), jnp.exp(l1 - lmax)
    lb = e0 / (e0 + e1)

    q = q_in * _sigmoid(q_in)
    f = lb + (1.0 - lb) * _sigmoid(f_in)
    k = 1.0 - f
    a = _chunk_cumsum(jnp.log(f))
    gate = g_in * _sigmoid(g_in)

    chunks = [slice(c * CH, (c + 1) * CH) for c in range(tm // CH)]
    qb, kb = q.astype(BF16), k.astype(BF16)
    scores = [jnp.where(eye, _dot_nt(qb[rows], kb[rows]), 0.0) for rows in chunks]
    for b, mask in zip(LEVEL_BLOCKS, masks):
        e = jnp.exp(-jnp.abs(a - _level_reference(a, b)))
        qe = (q * e).astype(BF16)
        ke = (k * e).astype(BF16)
        scores = [jnp.where(mask, _dot_nt(qe[rows], ke[rows]), s) for rows, s in zip(chunks, scores)]
    outs = []
    for rows, s in zip(chunks, scores):
        a_c = a[rows]
        a_last = a_c[CH - 1:CH, :]
        q0 = (q[rows] * jnp.exp(a_c)).astype(BF16)
        k_end = (k[rows] * jnp.exp(a_last - a_c)).astype(BF16)
        st = state_ref[hd]
        o = _dot(s.astype(BF16), v[rows]) + _dot_nt(q0, st.astype(BF16))
        state_ref[hd] = st * jnp.exp(a_last) + _dot_tn(v[rows], k_end)
        outs.append(_rms(o, hnorm) * gate[rows])
    return jnp.concatenate(outs, axis=0)


def _mix_kernel(x_ref, gain_ref, win_ref, band_ref, poolw_ref, pscale_ref, lbl_ref, hnorm_ref, wout_ref,
                o_ref, proj_ref, hn_ref, mix_ref, xkeep_ref, tail_ref, state_ref, *, per_batch):
    tm = x_ref.shape[0]
    i = pl.program_id(0)
    tile_in_seq = lax.rem(i - 1 + per_batch, per_batch)

    @pl.when(i == 0)
    def _nothing_to_finish_yet():
        proj_ref[...] = jnp.zeros_like(proj_ref)
        xkeep_ref[...] = jnp.zeros_like(xkeep_ref)

    @pl.when(tile_in_seq == 0)
    def _start_of_sequence():
        tail_ref[...] = jnp.zeros_like(tail_ref)
        state_ref[...] = jnp.zeros_like(state_ref)

    hn_ref[...] = _rms(x_ref[...], gain_ref[...]).astype(BF16)

    def project(cols):
        proj_ref[:, cols] = _dot(hn_ref[...], win_ref[:, cols])

    u = proj_ref[:, 0:POOL_DIM]
    u_ext = jnp.concatenate([tail_ref[...], u], axis=0).astype(BF16)
    tail_ref[...] = u[tm - HALO:, :]
    t_seq = tile_in_seq * tm + lax.broadcasted_iota(jnp.int32, (tm, 1), 0)
    for g, w in enumerate(POOL_WINDOWS):
        sl = slice(g * POOL_GROUP, (g + 1) * POOL_GROUP)
        sums = _dot(band_ref[g], u_ext[:, sl])
        count = jnp.minimum(t_seq + 1, w).astype(F32)
        pooled = sums / count - u[:, sl]
        out = _dot(pooled.astype(BF16), poolw_ref[g]) * pscale_ref[:, sl]
        mix_ref[:, sl] = out.astype(BF16)
    project(slice(0, POOL_DIM))
    o_ref[...] = xkeep_ref[...] + _dot(mix_ref[:, 0:POOL_DIM], wout_ref[0:POOL_DIM, :])

    ti = lax.broadcasted_iota(jnp.int32, (CH, CH), 0)
    si = lax.broadcasted_iota(jnp.int32, (CH, CH), 1)
    masks = [((ti ^ si) < b) & ((ti & (b // 2)) != 0) & ((si & (b // 2)) == 0) for b in LEVEL_BLOCKS]
    eye = ti == si
    for hd in range(HEADS):
        cols = slice(POOL_DIM + hd * HEAD_COLS, POOL_DIM + (hd + 1) * HEAD_COLS)
        head_out = _hgrn_head(proj_ref[:, cols], lbl_ref[:, hd * DK:(hd + 1) * DK], hnorm_ref[...],
                              state_ref, hd, masks, eye)
        mix_ref[:, POOL_DIM + hd * DK:POOL_DIM + (hd + 1) * DK] = head_out.astype(BF16)
        project(cols)
        if hd % 2 == 1:
            kk = slice(POOL_DIM + (hd - 1) * DK, POOL_DIM + (hd + 1) * DK)
            o_ref[...] += _dot(mix_ref[:, kk], wout_ref[kk, :])

    xkeep_ref[...] = x_ref[...]


def _pool_bands(tm):
    t = jnp.arange(tm)[:, None]
    c = jnp.arange(HALO + tm)[None, :] - HALO
    return jnp.stack([((c <= t) & (c > t - w)) for w in POOL_WINDOWS]).astype(BF16)


def _mixer(x, seq, norm_mix, w_in, pool_w, pool_scale, lb_logits, hgrn_norm, w_out):
    tokens = x.shape[0]
    tm = TM_MIX
    assert seq % tm == 0 and tm % CH == 0
    n_tiles = tokens // tm
    w_heads = w_in[:, POOL_DIM:].reshape(D_MODEL, 4, HEADS, DK).transpose(0, 2, 1, 3).reshape(D_MODEL, 4 * HGRN_DIM)
    w_in_grouped = jnp.concatenate([w_in[:, :POOL_DIM], w_heads], axis=1).astype(BF16)
    return pl.pallas_call(
        functools.partial(_mix_kernel, per_batch=seq // tm),
        grid=(n_tiles + 1,),
        in_specs=[
            pl.BlockSpec((tm, D_MODEL), lambda i: (jnp.minimum(i, n_tiles - 1), 0)),
            _const_spec((1, D_MODEL)),
            _const_spec((D_MODEL, IN_DIM)),
            _const_spec((len(POOL_WINDOWS), tm, HALO + tm)),
            _const_spec((len(POOL_WINDOWS), POOL_GROUP, POOL_GROUP)),
            _const_spec((1, POOL_DIM)),
            _const_spec((2, HGRN_DIM)),
            _const_spec((1, DK)),
            _const_spec((D_MODEL, D_MODEL)),
        ],
        out_specs=pl.BlockSpec((tm, D_MODEL), lambda i: (jnp.maximum(i - 1, 0), 0)),
        out_shape=jax.ShapeDtypeStruct(x.shape, F32),
        scratch_shapes=[
            pltpu.VMEM((tm, IN_DIM), F32),
            pltpu.VMEM((tm, D_MODEL), BF16),
            pltpu.VMEM((tm, D_MODEL), BF16),
            pltpu.VMEM((tm, D_MODEL), F32),
            pltpu.VMEM((HALO, POOL_DIM), F32),
            pltpu.VMEM((HEADS, DK, DK), F32),
        ],
        compiler_params=pltpu.CompilerParams(
            dimension_semantics=("arbitrary",), vmem_limit_bytes=V7X_VMEM_LIMIT_BYTES),
        name="mixer",
    )(x, norm_mix.reshape(1, D_MODEL), w_in_grouped, _pool_bands(tm), pool_w.astype(BF16),
      pool_scale.reshape(1, POOL_DIM), lb_logits, hgrn_norm.reshape(1, DK), w_out.astype(BF16))


def _kv_kernel(mem_ref, gain_ref, wk_ref, wv_ref, wq_ref, wo_ref, a_ref, b_ref):
    m = _rms(mem_ref[...], gain_ref[...]).astype(BF16)
    k = _dot(m, wk_ref[...]).astype(BF16)
    v = _dot(m, wv_ref[...]).astype(BF16)
    a_ref[...] = (_dot_nt(wq_ref[...], k) * (XHD ** -0.5)).astype(BF16)
    b_ref[...] = _dot(v, wo_ref[...]).astype(BF16)


def _kv_fold(mem, norm_mem, wq, wk, wv, wo):
    bsz, mlen, _ = mem.shape
    return pl.pallas_call(
        _kv_kernel,
        grid=(bsz, XH),
        in_specs=[
            pl.BlockSpec((None, mlen, D_MODEL), lambda b, h: (b, 0, 0)),
            pl.BlockSpec((1, D_MODEL), lambda b, h: (0, 0)),
            pl.BlockSpec((D_MODEL, XHD), lambda b, h: (0, h)),
            pl.BlockSpec((D_MODEL, XHD), lambda b, h: (0, h)),
            pl.BlockSpec((D_MODEL, XHD), lambda b, h: (0, h)),
            pl.BlockSpec((XHD, D_MODEL), lambda b, h: (h, 0)),
        ],
        out_specs=[
            pl.BlockSpec((None, D_MODEL, mlen), lambda b, h: (b, 0, h)),
            pl.BlockSpec((None, mlen, D_MODEL), lambda b, h: (b, h, 0)),
        ],
        out_shape=[
            jax.ShapeDtypeStruct((bsz, D_MODEL, XH * mlen), BF16),
            jax.ShapeDtypeStruct((bsz, XH * mlen, D_MODEL), BF16),
        ],
        compiler_params=pltpu.CompilerParams(
            dimension_semantics=("arbitrary", "arbitrary"), vmem_limit_bytes=V7X_VMEM_LIMIT_BYTES),
        name="kv_fold",
    )(mem, norm_mem.reshape(1, D_MODEL), wk.astype(BF16), wv.astype(BF16), wq.astype(BF16), wo.astype(BF16))


def _first_index_of(values, target):
    idx = jnp.full(target.shape, len(values) - 1, jnp.int32)
    for i in range(len(values) - 2, -1, -1):
        idx = jnp.where(values[i] == target, i, idx)
    return idx


def _xattn_kernel(x_ref, gain_ref, a_ref, b_ref, gain3_ref, wrt_ref, x2_ref, cls_ref, cnt_ref):
    mlen = a_ref.shape[1] // XH

    @pl.when(pl.program_id(0) == 0)
    def _first_tile():
        cnt_ref[...] = jnp.zeros_like(cnt_ref)

    x = x_ref[...]
    h = _rms(x, gain_ref[...]).astype(BF16)
    s = _dot(h, a_ref[...])
    probs = []
    for hd in range(XH):
        sh = s[:, hd * mlen:(hd + 1) * mlen]
        p = jnp.exp(sh - jnp.max(sh, axis=-1, keepdims=True))
        probs.append((p / jnp.sum(p, axis=-1, keepdims=True)).astype(BF16))
    x2 = x + _dot(jnp.concatenate(probs, axis=-1), b_ref[...])
    x2_ref[...] = x2

    h3 = _rms(x2, gain3_ref[...]).astype(BF16)
    lt = _dot_nt(wrt_ref[...], h3)
    gl = [lt[i:i + 1, :] for i in range(N_GROUPS)]
    gmax = functools.reduce(jnp.maximum, gl)
    gi = _first_index_of(gl, gmax)
    el = [lt[N_GROUPS + i:N_GROUPS + i + 1, :] for i in range(N_EXPERTS)]
    sel = []
    for i in range(EPG):
        v_i = el[(N_GROUPS - 1) * EPG + i]
        for g in range(N_GROUPS - 2, -1, -1):
            v_i = jnp.where(gi == g, el[g * EPG + i], v_i)
        sel.append(v_i)
    v1 = functools.reduce(jnp.maximum, sel)
    i1 = _first_index_of(sel, v1)
    rest = [jnp.where(i1 == i, -jnp.inf, sel[i]) for i in range(EPG)]
    v2 = functools.reduce(jnp.maximum, rest)
    i2 = _first_index_of(rest, v2)
    lo, hi = jnp.minimum(i1, i2), jnp.maximum(i1, i2)
    pair = jnp.where(lo == 0, hi - 1, jnp.where(lo == 1, hi + 1, PAIRS_PER_GROUP - 1))
    cls = gi * PAIRS_PER_GROUP + pair
    cls_ref[...] = jnp.broadcast_to(cls, cls_ref.shape)
    cid = lax.broadcasted_iota(jnp.int32, (CLASS_ROWS, cls.shape[1]), 0)
    per_class = jnp.sum((cid == cls).astype(F32), axis=-1, keepdims=True)
    cnt_ref[...] += jnp.broadcast_to(per_class, cnt_ref.shape)


def _xattn(x1, norm_xattn, a_fold, b_fold, norm_ffn, router_group, router_expert, seq):
    tokens = x1.shape[0]
    tm = TM_ATT
    per_batch = seq // tm
    kdim = a_fold.shape[2]
    wrt = jnp.zeros((ROUTER_ROWS, D_MODEL), F32)
    wrt = wrt.at[:N_GROUPS].set(router_group.T).at[N_GROUPS:N_GROUPS + N_EXPERTS].set(router_expert.T)
    return pl.pallas_call(
        _xattn_kernel,
        grid=(tokens // tm,),
        in_specs=[
            pl.BlockSpec((tm, D_MODEL), lambda i: (i, 0)),
            _const_spec((1, D_MODEL)),
            pl.BlockSpec((None, D_MODEL, kdim), lambda i: (i // per_batch, 0, 0)),
            pl.BlockSpec((None, kdim, D_MODEL), lambda i: (i // per_batch, 0, 0)),
            _const_spec((1, D_MODEL)),
            _const_spec((ROUTER_ROWS, D_MODEL)),
        ],
        out_specs=[
            pl.BlockSpec((tm, D_MODEL), lambda i: (i, 0)),
            pl.BlockSpec((SUBLANES, tm), lambda i: (0, i)),
            pl.BlockSpec((CLASS_ROWS, LANES), lambda i: (0, 0)),
        ],
        out_shape=[
            jax.ShapeDtypeStruct((tokens, D_MODEL), F32),
            jax.ShapeDtypeStruct((SUBLANES, tokens), jnp.int32),
            jax.ShapeDtypeStruct((CLASS_ROWS, LANES), F32),
        ],
        compiler_params=pltpu.CompilerParams(
            dimension_semantics=("arbitrary",), vmem_limit_bytes=V7X_VMEM_LIMIT_BYTES),
        name="xattn_router",
    )(x1, norm_xattn.reshape(1, D_MODEL), a_fold, b_fold, norm_ffn.reshape(1, D_MODEL), wrt.astype(BF16))


def _rank_kernel(cls_ref, off_ref, tri_ref, pos_ref, run_ref):
    @pl.when(pl.program_id(0) == 0)
    def _first_tile():
        run_ref[...] = jnp.zeros_like(run_ref)

    cls = cls_ref[0:1, :]
    tt = cls.shape[1]
    cid = lax.broadcasted_iota(jnp.int32, (CLASS_ROWS, tt), 0)
    onehot = cid == cls
    cum = _dot(onehot.astype(BF16), tri_ref[...])
    base = off_ref[:, 0:1] + run_ref[:, 0:1]
    pos = jnp.sum(jnp.where(onehot, cum - 1.0 + base, 0.0), axis=0, keepdims=True)
    pos_ref[...] = jnp.broadcast_to(pos.astype(jnp.int32), pos_ref.shape)
    run_ref[...] += jnp.broadcast_to(cum[:, tt - 1:tt], run_ref.shape)


def _rank(cls, offsets):
    tokens = cls.shape[1]
    tt = TT_RANK
    tri = (jnp.arange(tt)[:, None] <= jnp.arange(tt)[None, :]).astype(BF16)
    off = jnp.broadcast_to(offsets.astype(F32)[:, None], (CLASS_ROWS, LANES))
    return pl.pallas_call(
        _rank_kernel,
        grid=(tokens // tt,),
        in_specs=[
            pl.BlockSpec((SUBLANES, tt), lambda i: (0, i)),
            _const_spec((CLASS_ROWS, LANES)),
            _const_spec((tt, tt)),
        ],
        out_specs=pl.BlockSpec((SUBLANES, tt), lambda i: (0, i)),
        out_shape=jax.ShapeDtypeStruct((SUBLANES, tokens), jnp.int32),
        scratch_shapes=[pltpu.VMEM((CLASS_ROWS, LANES), F32)],
        compiler_params=pltpu.CompilerParams(dimension_semantics=("arbitrary",)),
        name="rank",
    )(cls, off, tri)


def _moe_kernel(elo_ref, ehi_ref, grp_ref, nvalid_ref, nused_ref,
                inv_ref, invn_ref, x_hbm, gain_ref, wr_ref, wg_lo, wu_lo, wd_lo, wg_hi, wu_hi, wd_hi, gfin_ref,
                out_hbm, xbuf, obuf, gsem, ssem):
    k = pl.program_id(0)
    n_used = nused_ref[0]
    tm = obuf.shape[0]
    slot = k % 2

    def start_gather(idx_ref, dst_slot):
        def one_row(r, c):
            pltpu.make_async_copy(x_hbm.at[pl.ds(idx_ref[0, r], 1)], xbuf.at[dst_slot, pl.ds(r, 1)],
                                  gsem.at[dst_slot]).start()
            return c
        lax.fori_loop(0, tm, one_row, 0, unroll=8)

    def wait_gather(dst_slot):
        def one_row(r, c):
            pltpu.make_async_copy(x_hbm.at[pl.ds(0, 1)], xbuf.at[dst_slot, pl.ds(0, 1)], gsem.at[dst_slot]).wait()
            return c
        lax.fori_loop(0, tm, one_row, 0, unroll=True)

    def start_scatter(n_rows):
        def one_row(r, c):
            pltpu.make_async_copy(obuf.at[pl.ds(r, 1)], out_hbm.at[pl.ds(inv_ref[0, r], 1)], ssem.at[0]).start()
            return c
        _full_or_partial(n_rows, one_row, unroll=8)

    def wait_scatter(n_rows):
        def one_row(r, c):
            pltpu.make_async_copy(obuf.at[pl.ds(0, 1)], out_hbm.at[pl.ds(0, 1)], ssem.at[0]).wait()
            return c
        _full_or_partial(n_rows, one_row, unroll=True)

    def _full_or_partial(n_rows, one_row, unroll):
        @pl.when(n_rows == tm)
        def _full_tile():
            lax.fori_loop(0, tm, one_row, 0, unroll=unroll)

        @pl.when(n_rows != tm)
        def _last_tile_of_a_class():
            lax.fori_loop(0, n_rows, one_row, 0)

    @pl.when(k < n_used)
    def _tile():
        @pl.when(k == 0)
        def _prologue():
            start_gather(inv_ref, 0)

        @pl.when(k + 1 < n_used)
        def _prefetch():
            start_gather(invn_ref, 1 - slot)

        wait_gather(slot)
        x = xbuf[slot]
        h = _rms(x, gain_ref[...]).astype(BF16)

        logits = _dot(h, wr_ref[...])
        lane = lax.broadcasted_iota(jnp.int32, logits.shape, 1)
        is_group = lane < N_GROUPS
        gmax = jnp.max(jnp.where(is_group, logits, -jnp.inf), axis=-1, keepdims=True)
        denom = jnp.sum(jnp.where(is_group, jnp.exp(logits - gmax), 0.0), axis=-1, keepdims=True)

        def pick(col):
            return jnp.sum(jnp.where(lane == col, logits, 0.0), axis=-1, keepdims=True)

        p_group = jnp.exp(pick(grp_ref[k]) - gmax) / denom
        v_lo = pick(N_GROUPS + elo_ref[k])
        v_hi = pick(N_GROUPS + ehi_ref[k])
        vmax = jnp.maximum(v_lo, v_hi)
        e_lo, e_hi = jnp.exp(v_lo - vmax), jnp.exp(v_hi - vmax)
        c_lo = p_group * (e_lo / (e_lo + e_hi))
        c_hi = p_group * (e_hi / (e_lo + e_hi))

        def expert(wg, wu, wd, c):
            gate = _dot(h, wg[...])
            hid = gate * _sigmoid(gate) * _dot(h, wu[...])
            return _dot((hid * c).astype(BF16), wd[...])

        y = expert(wg_lo, wu_lo, wd_lo, c_lo) + expert(wg_hi, wu_hi, wd_hi, c_hi)
        out = _rms(x + y, gfin_ref[...])

        @pl.when(k > 0)
        def _previous_scatter_done():
            wait_scatter(nvalid_ref[k - 1])

        obuf[...] = out
        start_scatter(nvalid_ref[k])

        @pl.when(k == n_used - 1)
        def _epilogue():
            wait_scatter(nvalid_ref[k])


def _moe(x2, cls, counts, norm_ffn, router_group, router_expert, w_gate, w_up, w_down, norm_final):
    tokens = x2.shape[0]
    tm = TM_MOE
    n_tiles = tokens // tm + N_CLASSES

    counts = counts[:N_CLASSES, 0].astype(jnp.int32)
    tiles_per_class = (counts + tm - 1) // tm
    tile_end = jnp.cumsum(tiles_per_class)
    tile_start = tile_end - tiles_per_class
    n_used = tile_end[-1]
    tile_ids = jnp.arange(n_tiles, dtype=jnp.int32)
    last_used = jnp.minimum(tile_ids, n_used - 1)
    tile_cls = jnp.sum((tile_end[None, :] <= last_used[:, None]).astype(jnp.int32), axis=1)
    tile_cls = jnp.minimum(tile_cls, N_CLASSES - 1)
    n_valid = jnp.clip(counts[tile_cls] - (tile_ids - tile_start[tile_cls]) * tm, 0, tm)
    n_valid = jnp.where(tile_ids < n_used, n_valid, 0).astype(jnp.int32)
    grp = tile_cls // PAIRS_PER_GROUP
    pair = tile_cls % PAIRS_PER_GROUP
    e_lo = (grp * EPG + jnp.asarray(PAIR_LO, jnp.int32)[pair]).astype(jnp.int32)
    e_hi = (grp * EPG + jnp.asarray(PAIR_HI, jnp.int32)[pair]).astype(jnp.int32)
    offsets = jnp.zeros((CLASS_ROWS,), jnp.int32).at[:N_CLASSES].set(tile_start * tm)

    pos = _rank(cls, offsets)[0]
    inv = jnp.zeros((n_tiles * tm,), jnp.int32).at[pos].set(jnp.arange(tokens, dtype=jnp.int32))
    inv = inv.reshape(n_tiles, 1, tm)

    wr = jnp.zeros((D_MODEL, LANES), F32)
    wr = wr.at[:, :N_GROUPS].set(router_group).at[:, N_GROUPS:N_GROUPS + N_EXPERTS].set(router_expert)
    wg, wu, wd = w_gate.astype(BF16), w_up.astype(BF16), w_down.astype(BF16)

    def lo_block(k, elo, ehi, g, nv, nu):
        return (elo[k], 0, 0)

    def hi_block(k, elo, ehi, g, nv, nu):
        return (ehi[k], 0, 0)

    def const2(k, *_):
        return (0, 0)

    up_spec = functools.partial(pl.BlockSpec, (None, D_MODEL, D_EXPERT))
    down_spec = functools.partial(pl.BlockSpec, (None, D_EXPERT, D_MODEL))
    grid_spec = pltpu.PrefetchScalarGridSpec(
        num_scalar_prefetch=5,
        grid=(n_tiles,),
        in_specs=[
            pl.BlockSpec((None, 1, tm), lambda k, *_: (k, 0, 0), memory_space=pltpu.SMEM),
            pl.BlockSpec((None, 1, tm), lambda k, *_: (jnp.minimum(k + 1, n_tiles - 1), 0, 0),
                         memory_space=pltpu.SMEM),
            pl.BlockSpec(memory_space=pl.ANY),
            pl.BlockSpec((1, D_MODEL), const2),
            pl.BlockSpec((D_MODEL, LANES), const2),
            up_spec(lo_block), up_spec(lo_block), down_spec(lo_block),
            up_spec(hi_block), up_spec(hi_block), down_spec(hi_block),
            pl.BlockSpec((1, D_MODEL), const2),
        ],
        out_specs=pl.BlockSpec(memory_space=pl.ANY),
        scratch_shapes=[
            pltpu.VMEM((2, tm, D_MODEL), F32),
            pltpu.VMEM((tm, D_MODEL), F32),
            pltpu.SemaphoreType.DMA((2,)),
            pltpu.SemaphoreType.DMA((1,)),
        ],
    )
    return pl.pallas_call(
        _moe_kernel,
        grid_spec=grid_spec,
        out_shape=jax.ShapeDtypeStruct((tokens, D_MODEL), F32),
        compiler_params=pltpu.CompilerParams(
            dimension_semantics=("arbitrary",), vmem_limit_bytes=V7X_VMEM_LIMIT_BYTES),
        name="moe_sorted",
    )(e_lo, e_hi, grp.astype(jnp.int32), n_valid, n_used.reshape(1).astype(jnp.int32),
      inv, inv, x2, norm_ffn.reshape(1, D_MODEL), wr.astype(BF16), wg, wu, wd, wg, wu, wd,
      norm_final.reshape(1, D_MODEL))


def kernel(x, mem, norm_mix, w_in, pool_w, pool_scale, hgrn_lb_logits, hgrn_norm, w_out, norm_xattn, norm_mem,
           xattn_wq, xattn_wk, xattn_wv, xattn_wo, norm_ffn, router_group, router_expert, w_gate, w_up, w_down,
           norm_final):
    bsz, seq, _ = x.shape
    assert norm_mix.shape[0] == 1, "one layer"
    x1 = _mixer(x.reshape(bsz * seq, D_MODEL), seq, norm_mix[0], w_in[0], pool_w[0], pool_scale[0],
                hgrn_lb_logits, hgrn_norm[0], w_out[0])
    a_fold, b_fold = _kv_fold(mem, norm_mem[0], xattn_wq[0], xattn_wk[0], xattn_wv[0], xattn_wo[0])
    x2, cls, counts = _xattn(x1, norm_xattn[0], a_fold, b_fold, norm_ffn[0], router_group[0], router_expert[0], seq)
    out = _moe(x2, cls, counts, norm_ffn[0], router_group[0], router_expert[0], w_gate[0], w_up[0], w_down[0],
               norm_final)
    return out.reshape(bsz, seq, D_MODEL)
```
